```python
import math
import jax, jax.numpy as jnp
from jax import lax
import numpy as np

D_MODEL = 2048
BATCH = 1
SEQ = 8192
DEPTH = 2
DEC_BATCH = 32
DEC_SEQ = 8
PAST_LEN = 8192
PAGE_SIZE = 128

MIX_W = D_MODEL // 2
N_BRANCH = 3
CONV_W = 3
D_CONV = MIX_W
HEAD_DIM = 128
N_HEADS = MIX_W // HEAD_DIM
N_KV_HEADS = N_HEADS // 2
MOBA_BLOCK = 256
MOBA_TOPK = 3
QUERY_BLOCK = 128
N_BUCKETS = 32
MAX_DISTANCE = 128
RET_DK = 256
RET_HEADS = MIX_W // RET_DK
RET_DV = MIX_W // RET_HEADS
RET_CHUNK = 128
ROPE_BASE = 10000.0
DENSE_FF = 5632
N_EXPERTS = 8
TOP_K = 2
EXPERT_FF = DENSE_FF // 2
NORM_EPS = 1e-6
GN_EPS = 1e-5
N_DENSE = (DEPTH + 1) // 2
N_MOE = DEPTH // 2
SPLITS = (D_CONV, D_CONV, D_CONV,
          N_HEADS * HEAD_DIM, N_KV_HEADS * HEAD_DIM, N_KV_HEADS * HEAD_DIM,
          RET_HEADS * RET_DK, RET_HEADS * RET_DK, RET_HEADS * RET_DV, RET_HEADS * RET_DV,
          N_BRANCH * D_MODEL)
IN_W = sum(SPLITS)

kernel_name = 'hybrid_conv_moba_retention_decoder_step'


def _rms_norm(x, g):
    xf = x.astype(jnp.float32)
    y = xf * lax.rsqrt(jnp.mean(xf * xf, axis=-1, keepdims=True) + NORM_EPS)
    return (y * g.astype(jnp.float32)).astype(x.dtype)


def _split_cols(p):
    outs, start = [], 0
    for w in SPLITS:
        outs.append(p[..., start:start + w])
        start += w
    return outs


def _short_conv(b_gate, c_gate, h, prev, w):
    n = h.shape[1]
    u = c_gate * h
    up = jnp.concatenate([prev.astype(u.dtype), u], axis=1)
    y = up[:, 0:n] * w[0]
    for t in range(1, CONV_W):
        y = y + up[:, t:t + n] * w[t]
    return b_gate * y, up[:, n:]


def _rel_bucket(rel):
    n = jnp.maximum(rel, 0)
    max_exact = N_BUCKETS // 2
    nf = jnp.maximum(n, max_exact).astype(jnp.float32)
    large = max_exact + (jnp.log(nf / max_exact) / math.log(MAX_DISTANCE / max_exact)
                         * (N_BUCKETS - max_exact)).astype(jnp.int32)
    large = jnp.minimum(large, N_BUCKETS - 1)
    return jnp.where(n < max_exact, n, large)


def _moba_sequence(q, k, v, rel_bias):
    nq, L = q.shape[0], k.shape[0]
    nblk = -(-L // MOBA_BLOCK)
    pad = nblk * MOBA_BLOCK - L
    k = jnp.pad(k, ((0, pad), (0, 0), (0, 0)))
    v = jnp.pad(v, ((0, pad), (0, 0), (0, 0)))
    kb = k.reshape(nblk, MOBA_BLOCK, N_KV_HEADS, HEAD_DIM).transpose(2, 0, 1, 3)
    vb = v.reshape(nblk, MOBA_BLOCK, N_KV_HEADS, HEAD_DIM).transpose(2, 0, 1, 3)
    kv_of_head = jnp.arange(N_HEADS) // (N_HEADS // N_KV_HEADS)
    kmean = jnp.mean(kb.astype(jnp.float32), axis=2)[kv_of_head]
    n_sel = min(MOBA_TOPK, nblk)
    qb = math.gcd(nq, QUERY_BLOCK)
    tpos = (L - nq) + jnp.arange(nq)
    bias_t = rel_bias.T.astype(jnp.float32)
    head_idx = jnp.arange(N_HEADS)
    scale = HEAD_DIM ** -0.5

    def one_block(args):
        qc, tc = args
        own = tc // MOBA_BLOCK
        scores = jnp.einsum('qhd,hjd->qhj', qc.astype(jnp.float32), kmean)
        fully_past = jnp.arange(nblk)[None, None, :] < own[:, None, None]
        scores = jnp.where(fully_past, scores, -jnp.inf)
        _, sel = lax.top_k(scores, n_sel)
        sel_ok = jnp.broadcast_to(jnp.arange(n_sel)[None, None, :] < own[:, None, None], sel.shape)
        own_b = jnp.broadcast_to(own[:, None, None], (qb, N_HEADS, 1)).astype(sel.dtype)
        blocks = jnp.concatenate([sel, own_b], axis=-1)
        blk_ok = jnp.concatenate([sel_ok, jnp.ones(own_b.shape, bool)], axis=-1)
        kg = kb[kv_of_head[None, :, None], blocks]
        vg = vb[kv_of_head[None, :, None], blocks]
        kpos = blocks[..., None] * MOBA_BLOCK + jnp.arange(MOBA_BLOCK)
        rel = tc[:, None, None, None] - kpos
        logits = (jnp.einsum('qhd,qhbkd->qhbk', qc, kg).astype(jnp.float32) * scale
                  + bias_t[head_idx[None, :, None, None], _rel_bucket(rel)])
        mask = blk_ok[..., None] & (rel >= 0)
        logits = jnp.where(mask, logits, -jnp.inf)
        p = jax.nn.softmax(logits.reshape(qb, N_HEADS, -1), axis=-1).reshape(logits.shape)
        return jnp.einsum('qhbk,qhbkd->qhd', p.astype(vg.dtype), vg)

    out = lax.map(one_block, (q.reshape(nq // qb, qb, N_HEADS, HEAD_DIM), tpos.reshape(nq // qb, qb)))
    return out.reshape(nq, N_HEADS, HEAD_DIM)


def _rope(x, pos):
    half = x.shape[-1] // 2
    inv = ROPE_BASE ** (-jnp.arange(half, dtype=jnp.float32) / half)
    ang = pos.astype(jnp.float32)[:, None] * inv
    cos = jnp.cos(ang)[None, :, None, :]
    sin = jnp.sin(ang)[None, :, None, :]
    x1, x2 = x[..., :half], x[..., half:]
    return jnp.concatenate([x1 * cos - x2 * sin, x1 * sin + x2 * cos], axis=-1)


def _retention(q, k, v, s0, pos0):
    b, n = q.shape[0], q.shape[1]
    pos = pos0 + jnp.arange(n)
    q = _rope(q.astype(jnp.float32), pos)
    k = _rope(k.astype(jnp.float32), pos) * (RET_DK ** -0.5)
    v = v.astype(jnp.float32)
    c = math.gcd(n, RET_CHUNK)
    nc = n // c
    log_g = jnp.log1p(-(2.0 ** (-5.0 - jnp.arange(RET_HEADS, dtype=jnp.float32))))
    i = jnp.arange(c, dtype=jnp.float32)
    diff = i[:, None] - i[None, :]
    decay = jnp.where(diff >= 0, jnp.exp(diff * log_g[:, None, None]), 0.0)
    q_decay = jnp.exp((i[None, :] + 1.0) * log_g[:, None]).T
    k_decay = jnp.exp((c - 1.0 - i)[None, :] * log_g[:, None]).T
    chunk_decay = jnp.exp(c * log_g)

    def to_chunks(t):
        return t.reshape(b, nc, c, t.shape[2], t.shape[3]).swapaxes(0, 1)

    def step(s, xs):
        qc, kc, vc = xs
        att = jnp.einsum('bihd,bjhd->bhij', qc, kc) * decay
        inner = jnp.einsum('bhij,bjhe->bihe', att, vc)
        cross = jnp.einsum('bihd,bhde->bihe', qc, s) * q_decay[None, :, :, None]
        s = s * chunk_decay[None, :, None, None] + jnp.einsum('bjhd,bjhe->bhde', kc * k_decay[None, :, :, None], vc)
        return s, inner + cross

    s_fin, o = lax.scan(step, s0.astype(jnp.float32), (to_chunks(q), to_chunks(k), to_chunks(v)))
    return o.swapaxes(0, 1).reshape(b, n, RET_HEADS, RET_DV), s_fin


def _mixer(x, pos0, conv_prev, ret_s0, k_past, v_past, rel_bias, g_mix, w_in, q_gain, k_gain,
           conv_w, w_branch, w_out):
    b, n, _ = x.shape
    xn = _rms_norm(x, g_mix)
    cb, cc, ch, q, k, v, rq, rk, rv, rg, gates = _split_cols(xn @ w_in)
    conv_out, conv_state = _short_conv(cb, cc, ch, conv_prev, conv_w)
    q = _rms_norm(q.reshape(b, n, N_HEADS, HEAD_DIM), q_gain)
    k = _rms_norm(k.reshape(b, n, N_KV_HEADS, HEAD_DIM), k_gain)
    v = v.reshape(b, n, N_KV_HEADS, HEAD_DIM)
    k_all = jnp.concatenate([k_past.astype(k.dtype), k], axis=1)
    v_all = jnp.concatenate([v_past.astype(v.dtype), v], axis=1)
    attn = lax.map(lambda a: _moba_sequence(a[0], a[1], a[2], rel_bias), (q, k_all, v_all))
    attn = attn.reshape(b, n, N_HEADS * HEAD_DIM)
    ro, ret_state = _retention(rq.reshape(b, n, RET_HEADS, RET_DK), rk.reshape(b, n, RET_HEADS, RET_DK),
                               rv.reshape(b, n, RET_HEADS, RET_DV), ret_s0, pos0)
    mu = jnp.mean(ro, axis=-1, keepdims=True)
    var = jnp.mean(jnp.square(ro - mu), axis=-1, keepdims=True)
    ro = (ro - mu) * lax.rsqrt(var + GN_EPS) * jax.nn.silu(rg.reshape(b, n, RET_HEADS, RET_DV).astype(jnp.float32))
    ro = ro.reshape(b, n, RET_HEADS * RET_DV).astype(x.dtype)
    branches = jnp.stack([conv_out, attn, ro], axis=2)
    proj = jnp.einsum('bnci,cid->bncd', branches, w_branch)
    gate = jax.nn.sigmoid(gates.reshape(b, n, N_BRANCH, D_MODEL).astype(jnp.float32)).astype(x.dtype)
    merged = jnp.sum(gate * proj, axis=2)
    return x + merged @ w_out, (conv_state, k, v, ret_state.astype(x.dtype))


def _swiglu(x, w1, w3, w2):
    return (jax.nn.silu(x @ w1) * (x @ w3)) @ w2


def _moe(x, w_r, b_r, w1, w3, w2):
    logits = (x @ w_r).astype(jnp.float32) + b_r.astype(jnp.float32)
    top_v, top_i = lax.top_k(logits, TOP_K)
    top_w = jax.nn.softmax(top_v, axis=-1)
    combine = jnp.sum(jax.nn.one_hot(top_i, N_EXPERTS, dtype=jnp.float32) * top_w[..., None], axis=-2)
    combine = combine.astype(x.dtype)
    out = jnp.zeros_like(x)
    for e in range(N_EXPERTS):
        out = out + combine[..., e:e + 1] * _swiglu(x, w1[e], w3[e], w2[e])
    return out


def setup_inputs(seed: int = 0) -> dict:
    key = jax.random.key(seed)
    ks = jax.random.split(key, 32)
    f32 = jnp.float32

    def nrm(k, shape, scale):
        return jax.random.normal(k, shape, f32) * scale

    n_pages = PAST_LEN // PAGE_SIZE
    n_pool = (DEC_BATCH * n_pages * 5) // 4
    page_table = jax.random.permutation(ks[2], n_pool)[:DEC_BATCH * n_pages].reshape(DEC_BATCH, n_pages).astype(jnp.int32)
    return {
        'x_prompt': nrm(ks[0], (BATCH, SEQ, D_MODEL), 1.0),
        'x_sample': nrm(ks[1], (DEC_BATCH, DEC_SEQ, D_MODEL), 1.0),
        'cache_k': nrm(ks[3], (DEPTH, n_pool, PAGE_SIZE, N_KV_HEADS, HEAD_DIM), 1.0),
        'cache_v': nrm(ks[4], (DEPTH, n_pool, PAGE_SIZE, N_KV_HEADS, HEAD_DIM), 1.0),
        'state_conv': nrm(ks[5], (DEPTH, DEC_BATCH, CONV_W - 1, D_CONV), 1.0),
        'state_ret': nrm(ks[6], (DEPTH, DEC_BATCH, RET_HEADS, RET_DK, RET_DV), 0.1),
        'page_table': page_table,
        'rel_bias': nrm(ks[7], (N_BUCKETS, N_HEADS), 0.1),
        'norm_mix': 1.0 + nrm(ks[8], (DEPTH, D_MODEL), 0.02),
        'w_in': nrm(ks[9], (DEPTH, D_MODEL, IN_W), D_MODEL ** -0.5),
        'q_norm': 1.0 + nrm(ks[10], (DEPTH, HEAD_DIM), 0.02),
        'k_norm': 1.0 + nrm(ks[11], (DEPTH, HEAD_DIM), 0.02),
        'conv_w': nrm(ks[12], (DEPTH, CONV_W, D_CONV), CONV_W ** -0.5),
        'w_branch': nrm(ks[13], (DEPTH, N_BRANCH, MIX_W, D_MODEL), MIX_W ** -0.5),
        'w_out': nrm(ks[14], (DEPTH, D_MODEL, D_MODEL), D_MODEL ** -0.5),
        'norm_ffn': 1.0 + nrm(ks[15], (DEPTH, D_MODEL), 0.02),
        'dense_w1': nrm(ks[16], (N_DENSE, D_MODEL, DENSE_FF), D_MODEL ** -0.5),
        'dense_w3': nrm(ks[17], (N_DENSE, D_MODEL, DENSE_FF), D_MODEL ** -0.5),
        'dense_w2': nrm(ks[18], (N_DENSE, DENSE_FF, D_MODEL), DENSE_FF ** -0.5),
        'moe_router': nrm(ks[19], (N_MOE, D_MODEL, N_EXPERTS), D_MODEL ** -0.5),
        'moe_router_b': nrm(ks[20], (N_MOE, N_EXPERTS), 0.01),
        'moe_w1': nrm(ks[21], (N_MOE, N_EXPERTS, D_MODEL, EXPERT_FF), D_MODEL ** -0.5),
        'moe_w3': nrm(ks[22], (N_MOE, N_EXPERTS, D_MODEL, EXPERT_FF), D_MODEL ** -0.5),
        'moe_w2': nrm(ks[23], (N_MOE, N_EXPERTS, EXPERT_FF, D_MODEL), EXPERT_FF ** -0.5),
    }


def reference(x_prompt, x_sample, cache_k, cache_v, state_conv, state_ret, page_table,
              rel_bias, norm_mix, w_in, q_norm, k_norm, conv_w, w_branch, w_out, norm_ffn,
              dense_w1, dense_w3, dense_w2, moe_router, moe_router_b, moe_w1, moe_w3, moe_w2):
    def run_layer(l, x, pos0, conv_prev, ret_s0, k_past, v_past):
        h, st = _mixer(x, pos0, conv_prev, ret_s0, k_past, v_past, rel_bias, norm_mix[l], w_in[l],
                       q_norm[l], k_norm[l], conv_w[l], w_branch[l], w_out[l])
        hn = _rms_norm(h, norm_ffn[l])
        j = l // 2
        if l % 2 == 0:
            f = _swiglu(hn, dense_w1[j], dense_w3[j], dense_w2[j])
        else:
            f = _moe(hn, moe_router[j], moe_router_b[j], moe_w1[j], moe_w3[j], moe_w2[j])
        return h + f, st

    b, db = x_prompt.shape[0], x_sample.shape[0]
    past_len = page_table.shape[1] * cache_k.shape[2]
    dt = x_prompt.dtype
    hp, hs = x_prompt, x_sample
    cp, kp, vp, rp, cs, ks, vs, rs = [], [], [], [], [], [], [], []
    for l in range(DEPTH):
        empty = jnp.zeros((b, 0, N_KV_HEADS, HEAD_DIM), dt)
        hp, (c1, k1, v1, r1) = run_layer(l, hp, 0, jnp.zeros((b, CONV_W - 1, D_CONV), dt),
                                         jnp.zeros((b, RET_HEADS, RET_DK, RET_DV), jnp.float32), empty, empty)
        k_past = cache_k[l][page_table].reshape(db, past_len, N_KV_HEADS, HEAD_DIM)
        v_past = cache_v[l][page_table].reshape(db, past_len, N_KV_HEADS, HEAD_DIM)
        hs, (c2, k2, v2, r2) = run_layer(l, hs, past_len, state_conv[l], state_ret[l], k_past, v_past)
        cp.append(c1); kp.append(k1); vp.append(v1); rp.append(r1)
        cs.append(c2); ks.append(k2); vs.append(v2); rs.append(r2)
    return (hp, hs, jnp.stack(cp), jnp.stack(kp), jnp.stack(vp), jnp.stack(rp),
            jnp.stack(cs), jnp.stack(ks), jnp.stack(vs), jnp.stack(rs))
```

```python
import functools
import math

import jax
import jax.numpy as jnp
import numpy as np
from jax import lax
from jax.experimental import pallas as pl
from jax.experimental.pallas import tpu as pltpu

F32 = jnp.float32
BF16 = jnp.bfloat16

D_MODEL = 2048
MIX_W = D_MODEL // 2
N_BRANCH = 3
CONV_W = 3
HEAD_DIM = 128
N_HEADS = MIX_W // HEAD_DIM
N_KV_HEADS = N_HEADS // 2
HEADS_PER_KV = N_HEADS // N_KV_HEADS
KV_W = N_KV_HEADS * HEAD_DIM
MOBA_BLOCK = 256
MOBA_TOPK = 3
N_BUCKETS = 32
MAX_DISTANCE = 128
RET_DK = 256
RET_HEADS = MIX_W // RET_DK
RET_DV = MIX_W // RET_HEADS
ROPE_BASE = 10000.0
N_EXPERTS = 8
TOP_K = 2
NORM_EPS = 1e-6
GN_EPS = 1e-5
NEG = -1e30
LANES = 128
VMEM_LIMIT = 56 * 1024 * 1024

OFF_CB, OFF_CC, OFF_CH = 0, MIX_W, 2 * MIX_W
OFF_Q = 3 * MIX_W
OFF_K = OFF_Q + MIX_W
OFF_V = OFF_K + KV_W
OFF_RQ = OFF_V + KV_W
OFF_RK = OFF_RQ + MIX_W
OFF_RV = OFF_RK + MIX_W
OFF_RG = OFF_RV + MIX_W
OFF_GATE = OFF_RG + MIX_W
IN_W = OFF_GATE + N_BRANCH * D_MODEL


def _cparams(*sem):
    return pltpu.CompilerParams(dimension_semantics=sem, vmem_limit_bytes=VMEM_LIMIT)


def _row_tile(t):
    for tm in (768, 512, 256, 128):
        if t % tm == 0:
            return tm
    raise ValueError(f"token count {t} is not a multiple of 128")


def _rms(x, g):
    return x * lax.rsqrt(jnp.mean(x * x, axis=-1, keepdims=True) + NORM_EPS) * g


def _dot(a, b):
    return jnp.dot(a, b, preferred_element_type=F32)


def _dot_nt(a, b):
    return lax.dot_general(a, b, (((1,), (1,)), ((), ())), preferred_element_type=F32)


def _dot_tn(a, b):
    return lax.dot_general(a, b, (((0,), (0,)), ((), ())), preferred_element_type=F32)


def _norm_matmul_kernel(x_ref, g_ref, w_ref, o_ref, xn_ref):
    @pl.when(pl.program_id(1) == 0)
    def _():
        xn_ref[...] = _rms(x_ref[...], g_ref[...]).astype(BF16)

    o_ref[...] = _dot(xn_ref[...], w_ref[...].astype(BF16))


def _norm_matmul(x, g, w, tn=512):
    t, d = x.shape
    n = w.shape[1]
    tm = _row_tile(t)
    return pl.pallas_call(
        _norm_matmul_kernel,
        grid=(t // tm, n // tn),
        in_specs=[pl.BlockSpec((tm, d), lambda i, j: (i, 0)),
                  pl.BlockSpec((1, d), lambda i, j: (0, 0)),
                  pl.BlockSpec((d, tn), lambda i, j: (0, j))],
        out_specs=pl.BlockSpec((tm, tn), lambda i, j: (i, j)),
        out_shape=jax.ShapeDtypeStruct((t, n), F32),
        scratch_shapes=[pltpu.VMEM((tm, d), BF16)],
        compiler_params=_cparams("parallel", "arbitrary"),
        name="norm_in_proj",
    )(x, g.reshape(1, d), w)


def _conv_kernel(cb_ref, cc_ref, ch_ref, w_ref, prev_ref, y_ref, stp_ref, sts_ref, carry_ref, *,
                 n_prompt_tiles, dec_seq):
    i = pl.program_id(0)
    w = w_ref[...]
    u = cc_ref[...] * ch_ref[...]
    tm, c = u.shape

    @pl.when(i == 0)
    def _():
        carry_ref[...] = jnp.zeros_like(carry_ref)

    @pl.when(i < n_prompt_tiles)
    def _():
        row = lax.broadcasted_iota(jnp.int32, u.shape, 0)
        tail = carry_ref[...]
        um1 = jnp.where(row == 0, tail[7:8], pltpu.roll(u, 1, 0))
        um2 = jnp.where(row == 0, tail[6:7], jnp.where(row == 1, tail[7:8], pltpu.roll(u, 2, 0)))
        y = cb_ref[...] * (um2 * w[0:1] + um1 * w[1:2] + u * w[2:3])
        y_ref[...] = y.astype(y_ref.dtype)
        carry_ref[...] = u[tm - 8:]
        stp_ref[...] = u[tm - 8:]

    @pl.when(i >= n_prompt_tiles)
    def _():
        nb = tm // dec_seq
        u3 = u.reshape(nb, dec_seq, c)
        row = lax.broadcasted_iota(jnp.int32, u3.shape, 1)
        prev = prev_ref[...]
        p0 = prev[:, 0:1, :]
        p1 = prev[:, 1:2, :]
        um1 = jnp.where(row == 0, p1, pltpu.roll(u3, 1, 1))
        um2 = jnp.where(row == 0, p0, jnp.where(row == 1, p1, pltpu.roll(u3, 2, 1)))
        y = um2 * w[0:1][None] + um1 * w[1:2][None] + u3 * w[2:3][None]
        y_ref[...] = (cb_ref[...] * y.reshape(tm, c)).astype(y_ref.dtype)
        sts_ref[...] = u3[:, dec_seq - 2:, :]


def _short_conv(p, w, prev_dec, n_prompt, dec_seq):
    t = p.shape[0]
    c = MIX_W
    tm = t - n_prompt
    assert n_prompt % tm == 0 and tm % 8 == 0 and dec_seq == 8
    npt = n_prompt // tm
    nb = tm // dec_seq
    col = lambda off: pl.BlockSpec((tm, c), lambda i, o=off // c: (i, o))
    y, stp, sts = pl.pallas_call(
        functools.partial(_conv_kernel, n_prompt_tiles=npt, dec_seq=dec_seq),
        grid=(npt + 1,),
        in_specs=[col(OFF_CB), col(OFF_CC), col(OFF_CH),
                  pl.BlockSpec((CONV_W, c), lambda i: (0, 0)),
                  pl.BlockSpec((nb, CONV_W - 1, c), lambda i: (0, 0, 0))],
        out_specs=[pl.BlockSpec((tm, c), lambda i: (i, 0)),
                   pl.BlockSpec((8, c), lambda i: (0, 0)),
                   pl.BlockSpec((nb, CONV_W - 1, c), lambda i: (0, 0, 0))],
        out_shape=[jax.ShapeDtypeStruct((t, c), BF16),
                   jax.ShapeDtypeStruct((8, c), F32),
                   jax.ShapeDtypeStruct((nb, CONV_W - 1, c), F32)],
        scratch_shapes=[pltpu.VMEM((8, c), F32)],
        compiler_params=_cparams("arbitrary"),
        name="short_conv",
    )(p, p, p, w, prev_dec)
    return y, stp[6:8][None], sts


def _qk_prep_kernel(q_ref, k_ref, v_ref, qg_ref, kg_ref, qn_ref, kn_ref, kb_ref, vb_ref, km_ref):
    qg = qg_ref[...]
    kg = kg_ref[...]
    q = q_ref[...]
    k = k_ref[...]
    qn = [_rms(q[:, h * HEAD_DIM:(h + 1) * HEAD_DIM], qg) for h in range(N_HEADS)]
    kn = [_rms(k[:, h * HEAD_DIM:(h + 1) * HEAD_DIM], kg) for h in range(N_KV_HEADS)]
    qn_ref[...] = jnp.concatenate(qn, axis=1).astype(BF16)
    kn = jnp.concatenate(kn, axis=1)
    kn_ref[...] = kn
    kb_ref[...] = kn.astype(BF16)
    vb_ref[...] = v_ref[...].astype(BF16)
    km_ref[0] = jnp.mean(kn, axis=0, keepdims=True)


def _qk_prep(p, qg, kg):
    t = p.shape[0]
    tm = MOBA_BLOCK
    nt = t // tm
    return pl.pallas_call(
        _qk_prep_kernel,
        grid=(nt,),
        in_specs=[pl.BlockSpec((tm, MIX_W), lambda i: (i, OFF_Q // MIX_W)),
                  pl.BlockSpec((tm, KV_W), lambda i: (i, OFF_K // KV_W)),
                  pl.BlockSpec((tm, KV_W), lambda i: (i, OFF_V // KV_W)),
                  pl.BlockSpec((1, HEAD_DIM), lambda i: (0, 0)),
                  pl.BlockSpec((1, HEAD_DIM), lambda i: (0, 0))],
        out_specs=[pl.BlockSpec((tm, MIX_W), lambda i: (i, 0)),
                   pl.BlockSpec((tm, KV_W), lambda i: (i, 0)),
                   pl.BlockSpec((tm, KV_W), lambda i: (i, 0)),
                   pl.BlockSpec((tm, KV_W), lambda i: (i, 0)),
                   pl.BlockSpec((1, 1, KV_W), lambda i: (i, 0, 0))],
        out_shape=[jax.ShapeDtypeStruct((t, MIX_W), BF16),
                   jax.ShapeDtypeStruct((t, KV_W), F32),
                   jax.ShapeDtypeStruct((t, KV_W), BF16),
                   jax.ShapeDtypeStruct((t, KV_W), BF16),
                   jax.ShapeDtypeStruct((nt, 1, KV_W), F32)],
        compiler_params=_cparams("parallel"),
        name="qk_prep",
    )(p, p, p, qg.reshape(1, HEAD_DIM), kg.reshape(1, HEAD_DIM))


def _top_blocks(sc, blk, axis):
    sel = jnp.zeros(sc.shape, F32)
    for _ in range(MOBA_TOPK):
        m = jnp.max(sc, axis=axis, keepdims=True)
        cand = jnp.where((sc == m) & (m > 0.5 * NEG), blk, 1 << 30)
        pick = blk == jnp.min(cand, axis=axis, keepdims=True)
        sel = jnp.where(pick, 1.0, sel)
        sc = jnp.where(pick, NEG, sc)
    return sel


def _moba_prompt_kernel(q_ref, k_ref, v_ref, km_ref, bias_ref, o_ref):
    qi = pl.program_id(1)
    blk_rows = MOBA_BLOCK
    scale = HEAD_DIM ** -0.5
    q2 = q_ref[...]
    q = jnp.concatenate([q2[:, h * HEAD_DIM:(h + 1) * HEAD_DIM] for h in range(HEADS_PER_KV)], axis=0)

    sc = lax.dot_general(q.astype(F32), km_ref[...], (((1,), (1,)), ((), ())),
                         preferred_element_type=F32, precision=lax.Precision.HIGHEST)
    blk = lax.broadcasted_iota(jnp.int32, sc.shape, 1)
    sel = _top_blocks(jnp.where(blk < qi, sc, NEG), blk, 1)

    def kv(j):
        start = pl.multiple_of(j * blk_rows, blk_rows)
        return k_ref[pl.ds(start, blk_rows), :], v_ref[pl.ds(start, blk_rows), :]

    kd, vd = kv(qi)
    s = _dot_nt(q, kd) * scale + bias_ref[0]
    m0 = jnp.max(s, axis=1, keepdims=True)
    p = jnp.exp(s - m0)
    l0 = jnp.sum(p, axis=1, keepdims=True)
    acc0 = _dot(p.astype(BF16), vd)

    def body(j, carry):
        m, l, acc = carry
        kj, vj = kv(j)
        s = _dot_nt(q, kj) * scale + bias_ref[jnp.minimum(qi - j, 2)]
        chosen = jnp.sum(jnp.where(blk == j, sel, 0.0), axis=1, keepdims=True) > 0.5
        s = jnp.where(chosen, s, NEG)
        m_new = jnp.maximum(m, jnp.max(s, axis=1, keepdims=True))
        alpha = jnp.exp(m - m_new)
        p = jnp.exp(s - m_new)
        l = alpha * l + jnp.sum(p, axis=1, keepdims=True)
        acc = alpha * acc + _dot(p.astype(BF16), vj)
        return m_new, l, acc

    _, l, acc = lax.fori_loop(0, qi, body, (m0, l0, acc0))
    out = acc / l
    o_ref[...] = jnp.concatenate([out[h * blk_rows:(h + 1) * blk_rows] for h in range(HEADS_PER_KV)],
                                 axis=1).astype(o_ref.dtype)


def _moba_prompt(qn, kb, vb, kmean, bias, n_prompt):
    t = qn.shape[0]
    nblk = n_prompt // MOBA_BLOCK
    gw = HEADS_PER_KV * HEAD_DIM
    return pl.pallas_call(
        _moba_prompt_kernel,
        grid=(N_KV_HEADS, nblk),
        in_specs=[pl.BlockSpec((MOBA_BLOCK, gw), lambda g, i: (i, g)),
                  pl.BlockSpec((n_prompt, HEAD_DIM), lambda g, i: (0, g)),
                  pl.BlockSpec((n_prompt, HEAD_DIM), lambda g, i: (0, g)),
                  pl.BlockSpec((nblk, HEAD_DIM), lambda g, i: (0, g)),
                  pl.BlockSpec((None, 3, HEADS_PER_KV * MOBA_BLOCK, MOBA_BLOCK), lambda g, i: (g, 0, 0, 0))],
        out_specs=pl.BlockSpec((MOBA_BLOCK, gw), lambda g, i: (i, g)),
        out_shape=jax.ShapeDtypeStruct((t, MIX_W), BF16),
        compiler_params=_cparams("parallel", "arbitrary"),
        name="moba_prompt",
    )(qn, kb, vb, kmean, bias)


def _moba_decode_kernel(pt_ref, q_ref, kc_ref, vc_ref, kn_ref, vn_ref, bias_ref, bown_ref, o_ref,
                        qbd_s, qf_s, ks_s, sc_s, m_s, l_s, o_s, *, n_pages, dec_seq):
    del pt_ref
    p = pl.program_id(1)
    rows = N_HEADS * dec_seq
    grp = HEADS_PER_KV * dec_seq
    scale = HEAD_DIM ** -0.5
    pages_per_blk = MOBA_BLOCK // kc_ref.shape[1]
    row_kv = lax.broadcasted_iota(jnp.int32, (rows, HEAD_DIM), 0) // grp

    def own_cols(x):
        out = jnp.zeros((rows, HEAD_DIM), F32)
        for g in range(N_KV_HEADS):
            out = jnp.where(row_kv == g, x[:, g * HEAD_DIM:(g + 1) * HEAD_DIM], out)
        return out

    @pl.when(p == 0)
    def _():
        q = q_ref[0].astype(F32)
        qbd_s[...] = jnp.zeros_like(qbd_s)
        for h in range(N_HEADS):
            qh = q[:, h * HEAD_DIM:(h + 1) * HEAD_DIM]
            g = h // HEADS_PER_KV
            qbd_s[h * dec_seq:(h + 1) * dec_seq, g * HEAD_DIM:(g + 1) * HEAD_DIM] = qh
            qf_s[h * dec_seq:(h + 1) * dec_seq, :] = qh

    kp = kc_ref[0]
    vp = vc_ref[0]
    qbd = qbd_s[...].astype(BF16)
    last = (p == n_pages - 1).astype(jnp.int32)
    s = _dot_nt(qbd, kp.astype(BF16)) * scale + bias_ref[last]
    m = jnp.max(s, axis=1, keepdims=True)
    e = jnp.exp(s - m)
    l = jnp.sum(e, axis=1, keepdims=True)
    o = own_cols(_dot(e.astype(BF16), vp.astype(BF16)))
    pb, pr = p // pages_per_blk, p % pages_per_blk
    m_s[pb, pr] = jnp.broadcast_to(m, (rows, HEAD_DIM))
    l_s[pb, pr] = jnp.broadcast_to(l, (rows, HEAD_DIM))
    o_s[pb, pr] = o

    ksum = jnp.sum(kp, axis=0, keepdims=True)

    @pl.when(p % pages_per_blk == 0)
    def _():
        ks_s[...] = jnp.broadcast_to(ksum, ks_s.shape)

    @pl.when(p % pages_per_blk != 0)
    def _():
        ks_s[...] = ks_s[...] + ksum

    @pl.when(p % pages_per_blk == pages_per_blk - 1)
    def _():
        kmean = ks_s[0:1, :] * (1.0 / MOBA_BLOCK)
        km_rows = own_cols(jnp.broadcast_to(kmean, (rows, KV_W)))
        score = jnp.sum(qf_s[...] * km_rows, axis=1, keepdims=True)
        sc_s[p // pages_per_blk] = jnp.broadcast_to(score, (rows, HEAD_DIM))

    @pl.when(p == n_pages - 1)
    def _():
        sc = sc_s[...]
        blk = lax.broadcasted_iota(jnp.int32, sc.shape, 0)
        selp = _top_blocks(sc, blk, 0)[:, None] > 0.5
        pad = jnp.zeros((kp.shape[0] - dec_seq, KV_W), F32)
        kn = jnp.concatenate([kn_ref[0], pad], axis=0).astype(BF16)
        vn = jnp.concatenate([vn_ref[0], pad], axis=0).astype(BF16)
        s_own = _dot_nt(qbd, kn) * scale + bown_ref[...]
        m_own = jnp.max(s_own, axis=1, keepdims=True)
        e_own = jnp.exp(s_own - m_own)
        l_own = jnp.sum(e_own, axis=1, keepdims=True)
        o_own = own_cols(_dot(e_own.astype(BF16), vn))
        mp = m_s[...]
        m_all = jnp.maximum(jnp.max(jnp.where(selp, mp, NEG), axis=(0, 1)), m_own)
        wgt = jnp.where(selp, jnp.exp(mp - m_all[None, None]), 0.0)
        w_own = jnp.exp(m_own - m_all)
        den = jnp.sum(wgt * l_s[...], axis=(0, 1)) + w_own * l_own
        num = jnp.sum(wgt * o_s[...], axis=(0, 1)) + w_own * o_own
        out = num / den
        for h in range(N_HEADS):
            o_ref[0, :, h * HEAD_DIM:(h + 1) * HEAD_DIM] = out[h * dec_seq:(h + 1) * dec_seq].astype(o_ref.dtype)


def _moba_decode(layer, page_table, qd, cache_k, cache_v, knew, vnew, bias_pages, bias_own):
    nb, dec_seq, _ = qd.shape
    n_pages = page_table.shape[1]
    n_pool, page = cache_k.shape[1], cache_k.shape[2]
    assert MOBA_BLOCK % page == 0 and (n_pages * page) % MOBA_BLOCK == 0 and dec_seq <= MOBA_BLOCK
    nblk = n_pages * page // MOBA_BLOCK
    rows = N_HEADS * dec_seq
    kc = cache_k.reshape(cache_k.shape[0] * n_pool, page, KV_W)
    vc = cache_v.reshape(cache_v.shape[0] * n_pool, page, KV_W)
    base = layer * n_pool
    cache_spec = pl.BlockSpec((1, page, KV_W), lambda b, p, pt: (base + pt[b * n_pages + p], 0, 0))
    seq_spec = lambda w: pl.BlockSpec((1, dec_seq, w), lambda b, p, pt: (b, 0, 0))
    grid_spec = pltpu.PrefetchScalarGridSpec(
        num_scalar_prefetch=1,
        grid=(nb, n_pages),
        in_specs=[seq_spec(MIX_W), cache_spec, cache_spec, seq_spec(KV_W), seq_spec(KV_W),
                  pl.BlockSpec((2, rows, page), lambda b, p, pt: (0, 0, 0)),
                  pl.BlockSpec((rows, page), lambda b, p, pt: (0, 0))],
        out_specs=seq_spec(MIX_W),
        scratch_shapes=[pltpu.VMEM((rows, KV_W), F32),
                        pltpu.VMEM((rows, HEAD_DIM), F32),
                        pltpu.VMEM((8, KV_W), F32),
                        pltpu.VMEM((nblk, rows, HEAD_DIM), F32),
                        pltpu.VMEM((nblk, MOBA_BLOCK // page, rows, HEAD_DIM), F32),
                        pltpu.VMEM((nblk, MOBA_BLOCK // page, rows, HEAD_DIM), F32),
                        pltpu.VMEM((nblk, MOBA_BLOCK // page, rows, HEAD_DIM), F32)],
    )
    return pl.pallas_call(
        functools.partial(_moba_decode_kernel, n_pages=n_pages, dec_seq=dec_seq),
        grid_spec=grid_spec,
        out_shape=jax.ShapeDtypeStruct((nb, dec_seq, MIX_W), BF16),
        compiler_params=_cparams("parallel", "arbitrary"),
        name="moba_decode",
    )(page_table.reshape(-1), qd, kc, vc, knew, vnew, bias_pages, bias_own)


def _retention_kernel(lg_ref, q_ref, k_ref, v_ref, g_ref, cos_ref, sin_ref, *rest, chunk, has_s0):
    if has_s0:
        s0_ref, o_ref, sout_ref, s_scr = rest
    else:
        o_ref, sout_ref, s_scr = rest
    c = pl.program_id(2)
    rows = q_ref.shape[0]
    half = RET_DK // 2

    @pl.when(c == 0)
    def _():
        s_scr[...] = s0_ref[0, 0] if has_s0 else jnp.zeros_like(s_scr)

    def padded(x):
        if rows == chunk:
            return x
        return jnp.concatenate([x, jnp.zeros((chunk - rows, x.shape[1]), x.dtype)], axis=0)

    lg = lg_ref[0, 0:1, 0:1]
    cos = padded(cos_ref[...])
    sin = padded(sin_ref[...])

    def rope(x):
        x1, x2 = x[:, :half], x[:, half:]
        return jnp.concatenate([x1 * cos - x2 * sin, x1 * sin + x2 * cos], axis=1)

    q = rope(padded(q_ref[...]))
    k = rope(padded(k_ref[...])) * (RET_DK ** -0.5)
    v = padded(v_ref[...]).astype(BF16)
    ri = lax.broadcasted_iota(jnp.int32, (chunk, chunk), 0)
    ci = lax.broadcasted_iota(jnp.int32, (chunk, chunk), 1)
    diff = (ri - ci).astype(F32)
    decay = jnp.where(diff >= 0, jnp.exp(diff * lg), 0.0)
    i = lax.broadcasted_iota(jnp.int32, (chunk, 1), 0).astype(F32)
    q_decay = jnp.exp((i + 1.0) * lg)
    k_decay = jnp.exp((rows - 1.0 - i) * lg)
    chunk_decay = jnp.exp(float(rows) * lg)

    qb = q.astype(BF16)
    att = _dot_nt(qb, k.astype(BF16)) * decay
    inner = _dot(att.astype(BF16), v)
    s = s_scr[...]
    cross = _dot(qb, s.astype(BF16)) * q_decay
    s_new = s * chunk_decay + _dot_tn((k * k_decay).astype(BF16), v)
    s_scr[...] = s_new

    o = (inner + cross)[:rows]
    mu = jnp.mean(o, axis=-1, keepdims=True)
    var = jnp.mean(jnp.square(o - mu), axis=-1, keepdims=True)
    g = g_ref[...]
    y = (o - mu) * lax.rsqrt(var + GN_EPS) * (g * jax.nn.sigmoid(g))
    o_ref[...] = y.astype(o_ref.dtype)

    @pl.when(c == pl.num_programs(2) - 1)
    def _():
        sout_ref[0, 0] = s_new


def _retention(p, log_g, cos, sin, s0, *, row0, n_seq, n_chunks, rows, out_rows, out_dtype):
    chunk = max(rows, LANES)
    rb0 = row0 // rows
    col = lambda off: pl.BlockSpec((rows, RET_DK), lambda b, h, c, o=off // RET_DK: (rb0 + b * n_chunks + c, o + h))
    tab = pl.BlockSpec((rows, RET_DK // 2), lambda b, h, c: (rb0 + b * n_chunks + c, 0))
    in_specs = [pl.BlockSpec((1, 8, LANES), lambda b, h, c: (h, 0, 0)),
                col(OFF_RQ), col(OFF_RK), col(OFF_RV), col(OFF_RG), tab, tab]
    args = [log_g, p, p, p, p, cos, sin]
    if s0 is not None:
        in_specs.append(pl.BlockSpec((1, 1, RET_DK, RET_DV), lambda b, h, c: (b, h, 0, 0)))
        args.append(s0)
    orb0 = 0 if out_rows != p.shape[0] else rb0
    return pl.pallas_call(
        functools.partial(_retention_kernel, chunk=chunk, has_s0=s0 is not None),
        grid=(n_seq, RET_HEADS, n_chunks),
        in_specs=in_specs,
        out_specs=[pl.BlockSpec((rows, RET_DV), lambda b, h, c: (orb0 + b * n_chunks + c, h)),
                   pl.BlockSpec((1, 1, RET_DK, RET_DV), lambda b, h, c: (b, h, 0, 0))],
        out_shape=[jax.ShapeDtypeStruct((out_rows, MIX_W), out_dtype),
                   jax.ShapeDtypeStruct((n_seq, RET_HEADS, RET_DK, RET_DV), F32)],
        scratch_shapes=[pltpu.VMEM((RET_DK, RET_DV), F32)],
        compiler_params=_cparams("parallel", "parallel", "arbitrary"),
        name="retention",
    )(*args)


def _merge_kernel(b0_ref, b1_ref, b2_ref, w_ref, g0_ref, g1_ref, g2_ref, o_ref):
    acc = None
    for c, (b_ref, g_ref) in enumerate(((b0_ref, g0_ref), (b1_ref, g1_ref), (b2_ref, g2_ref))):
        proj = _dot(b_ref[...], w_ref[c].astype(BF16))
        term = jax.nn.sigmoid(g_ref[...]) * proj
        acc = term if acc is None else acc + term
    o_ref[...] = acc.astype(o_ref.dtype)


def _merge(branches, w_branch, p, tn=512):
    t = p.shape[0]
    tm = _row_tile(t)
    br = pl.BlockSpec((tm, MIX_W), lambda i, j: (i, 0))
    gate = lambda c: pl.BlockSpec((tm, tn), lambda i, j, o=(OFF_GATE + c * D_MODEL) // tn: (i, o + j))
    return pl.pallas_call(
        _merge_kernel,
        grid=(t // tm, D_MODEL // tn),
        in_specs=[br, br, br,
                  pl.BlockSpec((N_BRANCH, MIX_W, tn), lambda i, j: (0, 0, j)),
                  gate(0), gate(1), gate(2)],
        out_specs=pl.BlockSpec((tm, tn), lambda i, j: (i, j)),
        out_shape=jax.ShapeDtypeStruct((t, D_MODEL), BF16),
        compiler_params=_cparams("parallel", "arbitrary"),
        name="branch_merge",
    )(*branches, w_branch, p, p, p)


def _out_proj_kernel(m_ref, w_ref, x_ref, o_ref):
    o_ref[...] = x_ref[...] + _dot(m_ref[...], w_ref[...].astype(BF16))


def _out_proj(merged, w_out, x, tn=512):
    t = x.shape[0]
    tm = _row_tile(t)
    return pl.pallas_call(
        _out_proj_kernel,
        grid=(t // tm, D_MODEL // tn),
        in_specs=[pl.BlockSpec((tm, D_MODEL), lambda i, j: (i, 0)),
                  pl.BlockSpec((D_MODEL, tn), lambda i, j: (0, j)),
                  pl.BlockSpec((tm, tn), lambda i, j: (i, j))],
        out_specs=pl.BlockSpec((tm, tn), lambda i, j: (i, j)),
        out_shape=jax.ShapeDtypeStruct((t, D_MODEL), F32),
        compiler_params=_cparams("parallel", "arbitrary"),
        name="out_proj",
    )(merged, w_out, x)


def _dense_ffn_kernel(x_ref, g_ref, w1_ref, w3_ref, w2_ref, o_ref, xn_ref):
    f = pl.program_id(1)

    @pl.when(f == 0)
    def _():
        x = x_ref[...]
        xn_ref[...] = _rms(x, g_ref[...]).astype(BF16)
        o_ref[...] = x

    xn = xn_ref[...]
    a = _dot(xn, w1_ref[...].astype(BF16))
    b = _dot(xn, w3_ref[...].astype(BF16))
    act = (a * jax.nn.sigmoid(a) * b).astype(BF16)
    o_ref[...] += _dot(act, w2_ref[...].astype(BF16))


def _dense_ffn(x, g, w1, w3, w2, tf=256):
    t, d = x.shape
    ff = w1.shape[1]
    tm = _row_tile(t)
    return pl.pallas_call(
        _dense_ffn_kernel,
        grid=(t // tm, ff // tf),
        in_specs=[pl.BlockSpec((tm, d), lambda i, f: (i, 0)),
                  pl.BlockSpec((1, d), lambda i, f: (0, 0)),
                  pl.BlockSpec((d, tf), lambda i, f: (0, f)),
                  pl.BlockSpec((d, tf), lambda i, f: (0, f)),
                  pl.BlockSpec((tf, d), lambda i, f: (f, 0))],
        out_specs=pl.BlockSpec((tm, d), lambda i, f: (i, 0)),
        out_shape=jax.ShapeDtypeStruct((t, d), F32),
        scratch_shapes=[pltpu.VMEM((tm, d), BF16)],
        compiler_params=_cparams("parallel", "arbitrary"),
        name="dense_ffn",
    )(x, g.reshape(1, d), w1, w3, w2)


def _moe_kernel(x_ref, g_ref, wr_ref, br_ref, w1_ref, w3_ref, w2_ref, o_ref, xn_ref, comb_ref, acc_ref):
    e = pl.program_id(1)
    f = pl.program_id(2)

    @pl.when((e == 0) & (f == 0))
    def _():
        x = x_ref[...]
        xn = _rms(x, g_ref[...])
        xn_ref[...] = xn.astype(BF16)
        o_ref[...] = x
        logits = jnp.dot(xn, wr_ref[...], preferred_element_type=F32,
                         precision=lax.Precision.HIGHEST) + br_ref[...]
        lane = lax.broadcasted_iota(jnp.int32, logits.shape, 1)
        v1 = jnp.max(logits, axis=1, keepdims=True)
        i1 = jnp.min(jnp.where(logits == v1, lane, LANES), axis=1, keepdims=True)
        rest = jnp.where(lane == i1, NEG, logits)
        v2 = jnp.max(rest, axis=1, keepdims=True)
        i2 = jnp.min(jnp.where(rest == v2, lane, LANES), axis=1, keepdims=True)
        e2 = jnp.exp(v2 - v1)
        den = 1.0 + e2
        comb_ref[...] = jnp.where(lane == i1, 1.0 / den, 0.0) + jnp.where(lane == i2, e2 / den, 0.0)

    xn = xn_ref[...]
    a = _dot(xn, w1_ref[0].astype(BF16))
    b = _dot(xn, w3_ref[0].astype(BF16))
    act = (a * jax.nn.sigmoid(a) * b).astype(BF16)
    part = _dot(act, w2_ref[0].astype(BF16))

    @pl.when(f == 0)
    def _():
        acc_ref[...] = part

    @pl.when(f != 0)
    def _():
        acc_ref[...] += part

    @pl.when(f == pl.num_programs(2) - 1)
    def _():
        comb = comb_ref[...]
        lane = lax.broadcasted_iota(jnp.int32, comb.shape, 1)
        wgt = jnp.sum(jnp.where(lane == e, comb, 0.0), axis=1, keepdims=True)
        o_ref[...] += wgt * acc_ref[...]


def _moe_ffn(x, g, w_r, b_r, w1, w3, w2, tf=256):
    t, d = x.shape
    ne, _, ff = w1.shape
    tm = _row_tile(t) // 2
    wr = jnp.zeros((d, LANES), F32).at[:, :ne].set(w_r)
    br = jnp.full((1, LANES), NEG, F32).at[0, :ne].set(b_r)
    return pl.pallas_call(
        _moe_kernel,
        grid=(t // tm, ne, ff // tf),
        in_specs=[pl.BlockSpec((tm, d), lambda i, e, f: (i, 0)),
                  pl.BlockSpec((1, d), lambda i, e, f: (0, 0)),
                  pl.BlockSpec((d, LANES), lambda i, e, f: (0, 0)),
                  pl.BlockSpec((1, LANES), lambda i, e, f: (0, 0)),
                  pl.BlockSpec((1, d, tf), lambda i, e, f: (e, 0, f)),
                  pl.BlockSpec((1, d, tf), lambda i, e, f: (e, 0, f)),
                  pl.BlockSpec((1, tf, d), lambda i, e, f: (e, f, 0))],
        out_specs=pl.BlockSpec((tm, d), lambda i, e, f: (i, 0)),
        out_shape=jax.ShapeDtypeStruct((t, d), F32),
        scratch_shapes=[pltpu.VMEM((tm, d), BF16),
                        pltpu.VMEM((tm, LANES), F32),
                        pltpu.VMEM((tm, d), F32)],
        compiler_params=_cparams("parallel", "arbitrary", "arbitrary"),
        name="moe_ffn",
    )(x, g.reshape(1, d), wr, br, w1, w3, w2)


def _rel_bucket(rel):
    n = jnp.maximum(rel, 0)
    max_exact = N_BUCKETS // 2
    nf = jnp.maximum(n, max_exact).astype(F32)
    large = max_exact + (jnp.log(nf / max_exact) / math.log(MAX_DISTANCE / max_exact)
                         * (N_BUCKETS - max_exact)).astype(jnp.int32)
    large = jnp.minimum(large, N_BUCKETS - 1)
    return jnp.where(n < max_exact, n, large)


def _bias_of_rel(bias_t, rel):
    b = bias_t[:, _rel_bucket(rel)]
    return jnp.where(rel[None] >= 0, b, NEG)


def _prompt_bias_tables(bias_t):
    r = jnp.arange(MOBA_BLOCK)[:, None]
    c = jnp.arange(MOBA_BLOCK)[None, :]
    tabs = jnp.stack([_bias_of_rel(bias_t, d * MOBA_BLOCK + r - c) for d in range(3)], axis=1)
    tabs = tabs.reshape(N_KV_HEADS, HEADS_PER_KV, 3, MOBA_BLOCK, MOBA_BLOCK).transpose(0, 2, 1, 3, 4)
    return tabs.reshape(N_KV_HEADS, 3, HEADS_PER_KV * MOBA_BLOCK, MOBA_BLOCK)


def _decode_bias_tables(bias_t, dec_seq, page):
    tq = jnp.arange(dec_seq)[:, None]
    ck = jnp.arange(page)[None, :]
    far = _bias_of_rel(bias_t, 2 * page + tq - ck)
    near = _bias_of_rel(bias_t, page + tq - ck)
    own = _bias_of_rel(bias_t, tq - ck)
    rows = N_HEADS * dec_seq
    return (jnp.stack([far.reshape(rows, page), near.reshape(rows, page)]), own.reshape(rows, page))


def _rope_tables(pos):
    half = RET_DK // 2
    inv = ROPE_BASE ** (-jnp.arange(half, dtype=F32) / half)
    ang = pos.astype(F32)[:, None] * inv
    return jnp.cos(ang), jnp.sin(ang)


def kernel(x_prompt, x_sample, cache_k, cache_v, state_conv, state_ret, page_table, rel_bias, norm_mix, w_in,
           q_norm, k_norm, conv_w, w_branch, w_out, norm_ffn, dense_w1, dense_w3, dense_w2, moe_router,
           moe_router_b, moe_w1, moe_w3, moe_w2):
    nb_p, n_prompt, d = x_prompt.shape
    nb_d, dec_seq, _ = x_sample.shape
    depth = w_in.shape[0]
    page = cache_k.shape[2]
    past_len = page_table.shape[1] * page
    n_dec = nb_d * dec_seq
    t = n_prompt + n_dec
    assert nb_p == 1 and d == D_MODEL and MAX_DISTANCE <= page and n_prompt % MOBA_BLOCK == 0
    ret_chunk = MOBA_BLOCK

    x = jnp.concatenate([x_prompt.reshape(n_prompt, d), x_sample.reshape(n_dec, d)], axis=0)
    bias_t = rel_bias.T.astype(F32)
    bias_prompt = _prompt_bias_tables(bias_t)
    bias_pages, bias_own = _decode_bias_tables(bias_t, dec_seq, page)
    pos = jnp.concatenate([jnp.arange(n_prompt), past_len + jnp.tile(jnp.arange(dec_seq), nb_d)])
    cos, sin = _rope_tables(pos)
    log_g = jnp.log1p(-(2.0 ** (-5.0 - jnp.arange(RET_HEADS, dtype=F32))))
    log_g = jnp.broadcast_to(log_g[:, None, None], (RET_HEADS, 8, LANES))

    outs = [[] for _ in range(8)]
    for l in range(depth):
        p = _norm_matmul(x, norm_mix[l], w_in[l])
        conv_y, conv_p, conv_s = _short_conv(p, conv_w[l], state_conv[l], n_prompt, dec_seq)

        qn, kn, kb, vb, kmean = _qk_prep(p, q_norm[l], k_norm[l])
        v = p[:, OFF_V:OFF_V + KV_W]
        kmean = kmean.reshape(-1, KV_W)[:n_prompt // MOBA_BLOCK]
        attn = _moba_prompt(qn, kb, vb, kmean, bias_prompt, n_prompt)
        attn_d = _moba_decode(l, page_table, qn[n_prompt:].reshape(nb_d, dec_seq, MIX_W), cache_k, cache_v,
                              kn[n_prompt:].reshape(nb_d, dec_seq, KV_W), v[n_prompt:].reshape(nb_d, dec_seq, KV_W),
                              bias_pages, bias_own)
        attn = lax.dynamic_update_slice(attn, attn_d.reshape(n_dec, MIX_W), (n_prompt, 0))

        ret_y, ret_p = _retention(p, log_g, cos, sin, None, row0=0, n_seq=1, n_chunks=n_prompt // ret_chunk,
                                  rows=ret_chunk, out_rows=t, out_dtype=BF16)
        ret_d, ret_s = _retention(p, log_g, cos, sin, state_ret[l], row0=n_prompt, n_seq=nb_d, n_chunks=1,
                                  rows=dec_seq, out_rows=n_dec, out_dtype=F32)
        ret_y = lax.dynamic_update_slice(ret_y, ret_d.astype(BF16), (n_prompt, 0))

        merged = _merge((conv_y, attn, ret_y), w_branch[l], p)
        h = _out_proj(merged, w_out[l], x)
        j = l // 2
        if l % 2 == 0:
            x = _dense_ffn(h, norm_ffn[l], dense_w1[j], dense_w3[j], dense_w2[j])
        else:
            x = _moe_ffn(h, norm_ffn[l], moe_router[j], moe_router_b[j], moe_w1[j], moe_w3[j], moe_w2[j])

        outs[0].append(conv_p)
        outs[1].append(kn[:n_prompt].reshape(1, n_prompt, N_KV_HEADS, HEAD_DIM))
        outs[2].append(v[:n_prompt].reshape(1, n_prompt, N_KV_HEADS, HEAD_DIM))
        outs[3].append(ret_p)
        outs[4].append(conv_s)
        outs[5].append(kn[n_prompt:].reshape(nb_d, dec_seq, N_KV_HEADS, HEAD_DIM))
        outs[6].append(v[n_prompt:].reshape(nb_d, dec_seq, N_KV_HEADS, HEAD_DIM))
        outs[7].append(ret_s)

    y_prompt = x[:n_prompt].reshape(1, n_prompt, d)
    y_sample = x[n_prompt:].reshape(nb_d, dec_seq, d)
    return (y_prompt, y_sample) + tuple(jnp.stack(o) for o in outs)
```
